```python
import math
import jax, jax.numpy as jnp
from jax import lax
import numpy as np

D_MODEL = 1024
BATCH = 8
SEQ = 2048
DEPTH = 1

HEAD_DIM = 64
MIX_WIDTH = D_MODEL
A_HEADS = (MIX_WIDTH // 2) // HEAD_DIM
A_KV_HEADS = 2
B_HEADS = (MIX_WIDTH - A_HEADS * HEAD_DIM) // HEAD_DIM
B_KV_HEADS = 2
A_W = A_HEADS * HEAD_DIM
A_KV_W = A_KV_HEADS * HEAD_DIM
B_W = B_HEADS * HEAD_DIM
B_KV_W = B_KV_HEADS * HEAD_DIM
IN_WIDTH = A_W + 2 * A_KV_W + B_W + 2 * B_KV_W
Q_BLOCK = 128
WINDOW = 128
GRID_W = 64
ROPE_THETA = 10000.0
REL_BUCKETS = 32
REL_MAX_DIST = 128
N_GROUPS = 4
EXPERTS_PER_GROUP = 8
TOP_K = 2
D_EXPERT = D_MODEL // 2
EPS = 1e-6
NEG_INF = -1e30

kernel_name = "hymba_axial_window_hmoe_encoder"


def rms_norm(x, g):
    xf = x.astype(jnp.float32)
    y = xf * lax.rsqrt(jnp.mean(xf * xf, axis=-1, keepdims=True) + EPS)
    return (y * g.astype(jnp.float32)).astype(x.dtype)


def axial_rope_tables(seq, dtype):
    rows = seq // GRID_W
    r = jnp.repeat(jnp.arange(rows), GRID_W).astype(jnp.float32)
    c = jnp.tile(jnp.arange(GRID_W), rows).astype(jnp.float32)
    half = HEAD_DIM // 2
    inv = ROPE_THETA ** (-jnp.arange(0, half, 2, dtype=jnp.float32) / half)
    ar = r[:, None] * inv[None, :]
    ac = c[:, None] * inv[None, :]
    ang = jnp.concatenate([ar, ar, ac, ac], axis=-1)
    return jnp.cos(ang)[:, None, :].astype(dtype), jnp.sin(ang)[:, None, :].astype(dtype)


def _rotate_half(u):
    u1, u2 = jnp.split(u, 2, axis=-1)
    return jnp.concatenate([-u2, u1], axis=-1)


def apply_axial_rope(x, cos, sin):
    xr, xc = jnp.split(x, 2, axis=-1)
    xrot = jnp.concatenate([_rotate_half(xr), _rotate_half(xc)], axis=-1)
    return x * cos + xrot * sin


def t5_bucket(rel):
    nb = REL_BUCKETS // 2
    ret = (rel > 0).astype(jnp.int32) * nb
    n = jnp.abs(rel)
    max_exact = nb // 2
    large = max_exact + (jnp.log(jnp.maximum(n, 1).astype(jnp.float32) / max_exact)
                         / math.log(REL_MAX_DIST / max_exact) * (nb - max_exact)).astype(jnp.int32)
    large = jnp.minimum(large, nb - 1)
    return ret + jnp.where(n < max_exact, n, large)


def mixer_global(q, k, v):
    b, s, _, _ = q.shape
    nb = s // Q_BLOCK
    g = A_HEADS // A_KV_HEADS
    qb = q.reshape(b, nb, Q_BLOCK, A_KV_HEADS, g, HEAD_DIM).transpose(1, 0, 2, 3, 4, 5)

    def block(qblk):
        sc = jnp.einsum('bqhgd,bkhd->bhgqk', qblk, k).astype(jnp.float32)
        p = jax.nn.softmax(sc, axis=-1).astype(v.dtype)
        return jnp.einsum('bhgqk,bkhd->bqhgd', p, v)

    o = lax.map(block, qb)
    return o.transpose(1, 0, 2, 3, 4, 5).reshape(b, s, A_W)


def window_blocks(t):
    b, s, h, d = t.shape
    nb = s // Q_BLOCK
    tp = jnp.pad(t, ((0, 0), (Q_BLOCK, Q_BLOCK), (0, 0), (0, 0)))
    tb = tp.reshape(b, nb + 2, Q_BLOCK, h, d)
    return jnp.concatenate([tb[:, :-2], tb[:, 1:-1], tb[:, 2:]], axis=2)


def mixer_window(q, k, v, sink, rel_bias):
    b, s, _, _ = q.shape
    nb = s // Q_BLOCK
    g = B_HEADS // B_KV_HEADS
    kq = 3 * Q_BLOCK
    qb = q.reshape(b, nb, Q_BLOCK, B_KV_HEADS, g, HEAD_DIM)
    kw = window_blocks(k)
    vw = window_blocks(v)
    rel = jnp.arange(kq)[None, :] - Q_BLOCK - jnp.arange(Q_BLOCK)[:, None]
    kpos = jnp.arange(nb)[:, None] * Q_BLOCK + jnp.arange(kq)[None, :] - Q_BLOCK
    valid = (jnp.abs(rel) <= WINDOW)[None] & ((kpos >= 0) & (kpos < s))[:, None, :]
    bias = rel_bias[t5_bucket(rel)].astype(jnp.float32)
    bias = bias.transpose(2, 0, 1).reshape(B_KV_HEADS, g, 1, Q_BLOCK, kq)
    sc = jnp.einsum('bnqhgd,bnkhd->bhgnqk', qb, kw).astype(jnp.float32) + bias
    sc = jnp.where(valid, sc, NEG_INF)
    sink_l = jnp.broadcast_to(sink.astype(jnp.float32).reshape(B_KV_HEADS, g, 1, 1, 1), sc.shape[:-1] + (1,))
    p = jax.nn.softmax(jnp.concatenate([sc, sink_l], axis=-1), axis=-1)[..., :-1].astype(v.dtype)
    o = jnp.einsum('bhgnqk,bnkhd->bnqhgd', p, vw)
    return o.reshape(b, s, B_W)


def hier_moe(x, w_group, b_group, w_expert, b_expert, w_gate, w_up, w_down):
    b, s, d = x.shape
    xt = x.reshape(-1, d)
    gl = (xt @ w_group + b_group).astype(jnp.float32)
    gp = jax.nn.softmax(gl, axis=-1)
    g_onehot = jax.nn.one_hot(jnp.argmax(gl, axis=-1), N_GROUPS, dtype=jnp.float32)
    g_w = jnp.sum(gp * g_onehot, axis=-1)
    el = (xt @ w_expert + b_expert).astype(jnp.float32).reshape(-1, N_GROUPS, EXPERTS_PER_GROUP)
    el_sel = jnp.einsum('nge,ng->ne', el, g_onehot)
    ep = jax.nn.softmax(el_sel, axis=-1)
    top_w, top_i = lax.top_k(ep, TOP_K)
    top_w = top_w / jnp.sum(top_w, axis=-1, keepdims=True)
    e_w = jnp.sum(jax.nn.one_hot(top_i, EXPERTS_PER_GROUP, dtype=jnp.float32) * top_w[..., None], axis=1)
    combine = (g_onehot[:, :, None] * (g_w[:, None] * e_w)[:, None, :]).astype(x.dtype)
    y = jnp.zeros_like(xt)
    for gi in range(N_GROUPS):
        h = jax.nn.silu(jnp.einsum('nd,edf->nef', xt, w_gate[gi])) * jnp.einsum('nd,edf->nef', xt, w_up[gi])
        y = y + jnp.einsum('nef,efd->nd', h * combine[:, gi, :, None], w_down[gi])
    return y.reshape(b, s, d)


def setup_inputs(seed: int = 0) -> dict:
    key = jax.random.key(seed)
    ks = jax.random.split(key, 20)
    L, D, F = DEPTH, D_MODEL, D_EXPERT
    G, E = N_GROUPS, EXPERTS_PER_GROUP
    nrm = lambda k, shape, scale: jax.random.normal(k, shape, jnp.float32) * scale
    return {
        "x": nrm(ks[0], (BATCH, SEQ, D), 1.0),
        "norm1_g": 1.0 + nrm(ks[1], (L, D), 0.02),
        "w_in": nrm(ks[2], (L, D, IN_WIDTH), D ** -0.5),
        "qn_g": 1.0 + nrm(ks[3], (L, HEAD_DIM), 0.02),
        "kn_g": 1.0 + nrm(ks[4], (L, HEAD_DIM), 0.02),
        "sink": nrm(ks[5], (L, B_HEADS), 0.5),
        "rel_bias": nrm(ks[6], (REL_BUCKETS, B_HEADS), 0.5),
        "gnorm_a": 1.0 + nrm(ks[7], (L, A_W), 0.02),
        "gnorm_b": 1.0 + nrm(ks[8], (L, B_W), 0.02),
        "w_o": nrm(ks[9], (L, MIX_WIDTH, D), MIX_WIDTH ** -0.5),
        "norm2_g": 1.0 + nrm(ks[10], (L, D), 0.02),
        "w_group": nrm(ks[11], (L, D, G), D ** -0.5),
        "b_group": nrm(ks[12], (L, G), 0.01),
        "w_expert": nrm(ks[13], (L, D, G * E), D ** -0.5),
        "b_expert": nrm(ks[14], (L, G * E), 0.01),
        "w_gate": nrm(ks[15], (L, G, E, D, F), D ** -0.5),
        "w_up": nrm(ks[16], (L, G, E, D, F), D ** -0.5),
        "w_down": nrm(ks[17], (L, G, E, F, D), F ** -0.5),
        "final_g": 1.0 + nrm(ks[18], (D,), 0.02),
    }


def reference(x, norm1_g, w_in, qn_g, kn_g, sink, rel_bias, gnorm_a, gnorm_b, w_o, norm2_g,
              w_group, b_group, w_expert, b_expert, w_gate, w_up, w_down, final_g):
    b, s, _ = x.shape
    cos, sin = axial_rope_tables(s, x.dtype)
    scale = HEAD_DIM ** -0.5
    splits = np.cumsum([A_W, A_KV_W, A_KV_W, B_W, B_KV_W]).tolist()
    h = x
    for l in range(DEPTH):
        n = rms_norm(h, norm1_g[l])
        p = n @ w_in[l]
        qa, ka, va, qb, kb, vb = jnp.split(p, splits, axis=-1)
        qa = qa.reshape(b, s, A_HEADS, HEAD_DIM)
        ka = ka.reshape(b, s, A_KV_HEADS, HEAD_DIM)
        va = va.reshape(b, s, A_KV_HEADS, HEAD_DIM)
        qa = apply_axial_rope(rms_norm(qa, qn_g[l]), cos, sin) * scale
        ka = apply_axial_rope(rms_norm(ka, kn_g[l]), cos, sin)
        oa = mixer_global(qa, ka, va)
        qb = qb.reshape(b, s, B_HEADS, HEAD_DIM) * scale
        kb = kb.reshape(b, s, B_KV_HEADS, HEAD_DIM)
        vb = vb.reshape(b, s, B_KV_HEADS, HEAD_DIM)
        ob = mixer_window(qb, kb, vb, sink[l], rel_bias)
        mix = jnp.concatenate([rms_norm(oa, gnorm_a[l]), rms_norm(ob, gnorm_b[l])], axis=-1)
        h = h + mix @ w_o[l]
        h = h + hier_moe(rms_norm(h, norm2_g[l]), w_group[l], b_group[l], w_expert[l], b_expert[l],
                         w_gate[l], w_up[l], w_down[l])
    return rms_norm(h, final_g)
```

```python
import functools
import math

import numpy as np
import jax
import jax.numpy as jnp
from jax import lax
from jax.experimental import pallas as pl
from jax.experimental.pallas import tpu as pltpu

F32 = jnp.float32
BF16 = jnp.bfloat16

HEAD_DIM = 64
A_HEADS = 8
B_HEADS = 8
KV_HEADS = 2
Q_PER_KV = 4
Q_BLOCK = 128
WINDOW = 128
GRID_W = 64
ROPE_THETA = 10000.0
REL_BUCKETS = 32
REL_MAX_DIST = 128
N_GROUPS = 4
EXPERTS_PER_GROUP = 8
N_EXPERTS = N_GROUPS * EXPERTS_PER_GROUP
EPS = 1e-6
NEG_INF = -1e30
LANES = 128

TOKEN_TILE = 512
ATTN_Q_TILE = 256
EXPERT_TILE = 256
VMEM_LIMIT = 48 * 1024 * 1024


def _cparams(n_axes):
    return pltpu.CompilerParams(dimension_semantics=("arbitrary",) * n_axes,
                                vmem_limit_bytes=VMEM_LIMIT)


def _head_norm_rope(u, gain, cos, sin_signed, blockdiag):
    sq = u * u
    sq_hi = sq.astype(BF16)
    sq_lo = (sq - sq_hi.astype(F32)).astype(BF16)
    ms = (jnp.dot(sq_hi, blockdiag, preferred_element_type=F32)
          + jnp.dot(sq_lo, blockdiag, preferred_element_type=F32))
    y = u * lax.rsqrt(ms + EPS) * gain
    lane = lax.broadcasted_iota(jnp.int32, y.shape, 1)
    first_half = (lane % 32) < 16
    rot = jnp.where(first_half, pltpu.roll(y, LANES - 16, 1), pltpu.roll(y, 16, 1))
    return y * cos + rot * sin_signed


def _inproj_kernel(x_ref, g_ref, w_ref, cos_ref, sin_ref, qg_ref, kg_ref, bd_ref,
                   qa_ref, kat_ref, va_ref, qb_ref, kbt_ref, vb_ref):
    x = x_ref[...]
    ms = jnp.mean(x * x, axis=-1, keepdims=True)
    n = (x * lax.rsqrt(ms + EPS) * g_ref[...]).astype(BF16)
    cos = cos_ref[...]
    sin_s = sin_ref[...]
    bd = bd_ref[...]
    qg = qg_ref[...]
    kg = kg_ref[...]
    scale = HEAD_DIM ** -0.5

    def proj(lo, width):
        return jnp.dot(n, w_ref[:, lo:lo + width], preferred_element_type=F32)

    for j in range(4):
        u = proj(j * LANES, LANES)
        qa_ref[:, j * LANES:(j + 1) * LANES] = (
            _head_norm_rope(u, qg, cos, sin_s, bd) * scale).astype(BF16)
    for h in range(KV_HEADS):
        u = proj(512 + h * LANES, LANES)
        k = _head_norm_rope(u, kg, cos, sin_s, bd)
        kat_ref[0, h] = k.T.astype(BF16)
        va_ref[0, h] = proj(768 + h * LANES, LANES).astype(BF16)
    qb_ref[...] = (proj(1024, 512) * scale).astype(BF16)
    for h in range(KV_HEADS):
        kbt_ref[0, h] = proj(1536 + h * LANES, LANES).T.astype(BF16)
        vb_ref[0, h] = proj(1792 + h * LANES, LANES).astype(BF16)


def _inproj(x2, g1, w_cat, cos, sin_s, qg, kg, bd, batch, seq):
    n_tok, d = x2.shape
    tm = TOKEN_TILE
    tpb = seq // tm
    tok = lambda i: (i, 0)
    const = lambda i: (0, 0)
    kt_spec = pl.BlockSpec((1, KV_HEADS, LANES, tm), lambda i: (i // tpb, 0, 0, i % tpb))
    v_spec = pl.BlockSpec((1, KV_HEADS, tm, LANES), lambda i: (i // tpb, 0, i % tpb, 0))
    kt_shape = jax.ShapeDtypeStruct((batch, KV_HEADS, LANES, seq), BF16)
    v_shape = jax.ShapeDtypeStruct((batch, KV_HEADS, seq, LANES), BF16)
    q_shape = jax.ShapeDtypeStruct((n_tok, 512), BF16)
    return pl.pallas_call(
        _inproj_kernel,
        grid=(n_tok // tm,),
        in_specs=[
            pl.BlockSpec((tm, d), tok),
            pl.BlockSpec((1, d), const),
            pl.BlockSpec(w_cat.shape, const),
            pl.BlockSpec((tm, LANES), lambda i: (i % tpb, 0)),
            pl.BlockSpec((tm, LANES), lambda i: (i % tpb, 0)),
            pl.BlockSpec((1, LANES), const),
            pl.BlockSpec((1, LANES), const),
            pl.BlockSpec((LANES, LANES), const),
        ],
        out_specs=[pl.BlockSpec((tm, 512), tok), kt_spec, v_spec,
                   pl.BlockSpec((tm, 512), tok), kt_spec, v_spec],
        out_shape=[q_shape, kt_shape, v_shape, q_shape, kt_shape, v_shape],
        compiler_params=_cparams(1),
        name="inproj",
    )(x2, g1, w_cat, cos, sin_s, qg, kg, bd)


def _group_rms(o, gain):
    ms = jnp.mean(o * o, axis=-1, keepdims=True)
    return o * lax.rsqrt(ms + EPS) * gain


def _pair_masks(rows):
    lane = lax.broadcasted_iota(jnp.int32, (rows, LANES), 1)
    return lane < HEAD_DIM


def _attn_a_kernel(q_ref, kt_ref, v_ref, g_ref, o_ref):
    tq = q_ref.shape[0]
    low = _pair_masks(tq)
    outs = []
    for h in range(KV_HEADS):
        kt = kt_ref[0, h]
        v = v_ref[0, h]
        for pr in range(2):
            qp = q_ref[:, (2 * h + pr) * LANES:(2 * h + pr + 1) * LANES]
            res = []
            for e in range(2):
                qm = jnp.where(low if e == 0 else jnp.logical_not(low), qp, jnp.zeros_like(qp))
                s = jnp.dot(qm, kt, preferred_element_type=F32)
                m = jnp.max(s, axis=-1, keepdims=True)
                p = jnp.exp(s - m)
                l = jnp.sum(p, axis=-1, keepdims=True)
                o = jnp.dot(p.astype(BF16), v, preferred_element_type=F32)
                res.append(o / l)
            outs.append(jnp.where(low, res[0], res[1]))
    oa = jnp.concatenate(outs, axis=1)
    o_ref[...] = _group_rms(oa, g_ref[...]).astype(BF16)


def _attn_a(qa, kat, va, gain, batch, seq):
    tq = ATTN_Q_TILE
    nq = seq // tq
    return pl.pallas_call(
        _attn_a_kernel,
        grid=(batch, nq),
        in_specs=[
            pl.BlockSpec((tq, 512), lambda b, j: (b * nq + j, 0)),
            pl.BlockSpec((1, KV_HEADS, LANES, seq), lambda b, j: (b, 0, 0, 0)),
            pl.BlockSpec((1, KV_HEADS, seq, LANES), lambda b, j: (b, 0, 0, 0)),
            pl.BlockSpec((1, 512), lambda b, j: (0, 0)),
        ],
        out_specs=pl.BlockSpec((tq, 512), lambda b, j: (b * nq + j, 0)),
        out_shape=jax.ShapeDtypeStruct(qa.shape, BF16),
        compiler_params=_cparams(2),
        name="attn_a",
    )(qa, kat, va, gain)


def _attn_b_kernel(sink_ref, q_ref, ktp_ref, ktc_ref, ktn_ref, vp_ref, vc_ref, vn_ref,
                   bias_ref, g_ref, o_ref, *, n_blocks):
    j = pl.program_id(1)
    tq = Q_BLOCK
    kq = 3 * Q_BLOCK
    low = _pair_masks(tq)
    row = lax.broadcasted_iota(jnp.int32, (tq, kq), 0)
    col = lax.broadcasted_iota(jnp.int32, (tq, kq), 1)
    rel = col - Q_BLOCK - row
    kblk = j - 1 + col // Q_BLOCK
    valid = (jnp.abs(rel) <= WINDOW) & (kblk >= 0) & (kblk < n_blocks)
    outs = []
    for h in range(KV_HEADS):
        kt = jnp.concatenate([ktp_ref[0, h], ktc_ref[0, h], ktn_ref[0, h]], axis=1)
        v = jnp.concatenate([vp_ref[0, h], vc_ref[0, h], vn_ref[0, h]], axis=0)
        for pr in range(2):
            qp = q_ref[:, (2 * h + pr) * LANES:(2 * h + pr + 1) * LANES]
            res = []
            for e in range(2):
                hq = 4 * h + 2 * pr + e
                qm = jnp.where(low if e == 0 else jnp.logical_not(low), qp, jnp.zeros_like(qp))
                s = jnp.dot(qm, kt, preferred_element_type=F32) + bias_ref[hq]
                s = jnp.where(valid, s, NEG_INF)
                sink = sink_ref[hq]
                m = jnp.maximum(jnp.max(s, axis=-1, keepdims=True), sink)
                p = jnp.exp(s - m)
                l = jnp.sum(p, axis=-1, keepdims=True) + jnp.exp(sink - m)
                o = jnp.dot(p.astype(BF16), v, preferred_element_type=F32)
                res.append(o / l)
            outs.append(jnp.where(low, res[0], res[1]))
    ob = jnp.concatenate(outs, axis=1)
    o_ref[...] = _group_rms(ob, g_ref[...]).astype(BF16)


def _attn_b(qb, kbt, vb, sink, bias, gain, batch, seq):
    nb = seq // Q_BLOCK
    prev = lambda j: jnp.maximum(j - 1, 0)
    nxt = lambda j: jnp.minimum(j + 1, nb - 1)
    kt_blk = (1, KV_HEADS, LANES, Q_BLOCK)
    v_blk = (1, KV_HEADS, Q_BLOCK, LANES)
    grid_spec = pltpu.PrefetchScalarGridSpec(
        num_scalar_prefetch=0,
        grid=(batch, nb),
        in_specs=[
            pl.BlockSpec(memory_space=pltpu.SMEM),
            pl.BlockSpec((Q_BLOCK, 512), lambda b, j: (b * nb + j, 0)),
            pl.BlockSpec(kt_blk, lambda b, j: (b, 0, 0, prev(j))),
            pl.BlockSpec(kt_blk, lambda b, j: (b, 0, 0, j)),
            pl.BlockSpec(kt_blk, lambda b, j: (b, 0, 0, nxt(j))),
            pl.BlockSpec(v_blk, lambda b, j: (b, 0, prev(j), 0)),
            pl.BlockSpec(v_blk, lambda b, j: (b, 0, j, 0)),
            pl.BlockSpec(v_blk, lambda b, j: (b, 0, nxt(j), 0)),
            pl.BlockSpec(bias.shape, lambda b, j: (0, 0, 0)),
            pl.BlockSpec((1, 512), lambda b, j: (0, 0)),
        ],
        out_specs=pl.BlockSpec((Q_BLOCK, 512), lambda b, j: (b * nb + j, 0)),
    )
    return pl.pallas_call(
        functools.partial(_attn_b_kernel, n_blocks=nb),
        grid_spec=grid_spec,
        out_shape=jax.ShapeDtypeStruct(qb.shape, BF16),
        compiler_params=_cparams(2),
        name="attn_b",
    )(sink, qb, kbt, kbt, kbt, vb, vb, vb, bias, gain)


def _outproj_kernel(ma_ref, mb_ref, wo_ref, x_ref, g_ref, wrh_ref, wrl_ref, br_ref,
                    h_ref, xn_ref, route_ref):
    half = ma_ref.shape[1]
    h = (x_ref[...]
         + jnp.dot(ma_ref[...], wo_ref[0:half, :], preferred_element_type=F32)
         + jnp.dot(mb_ref[...], wo_ref[half:2 * half, :], preferred_element_type=F32))
    h_ref[...] = h
    ms = jnp.mean(h * h, axis=-1, keepdims=True)
    xn = h * lax.rsqrt(ms + EPS) * g_ref[...]
    xn_hi = xn.astype(BF16)
    xn_ref[...] = xn_hi
    xn_lo = (xn - xn_hi.astype(F32)).astype(BF16)
    wrh = wrh_ref[...]
    lg = (jnp.dot(xn_hi, wrh, preferred_element_type=F32)
          + jnp.dot(xn_lo, wrh, preferred_element_type=F32)
          + jnp.dot(xn_hi, wrl_ref[...], preferred_element_type=F32)
          + br_ref[...])

    lane = lax.broadcasted_iota(jnp.int32, lg.shape, 1).astype(F32)
    ninf = jnp.float32(-jnp.inf)
    big = jnp.float32(LANES)
    rmax = lambda a: jnp.max(a, axis=-1, keepdims=True)
    rmin = lambda a: jnp.min(a, axis=-1, keepdims=True)

    gl = jnp.where(lane < N_GROUPS, lg, ninf)
    gmax = rmax(gl)
    gidx = rmin(jnp.where(gl == gmax, lane, big))
    g_w = 1.0 / jnp.sum(jnp.exp(gl - gmax), axis=-1, keepdims=True)
    lo = N_GROUPS + EXPERTS_PER_GROUP * gidx
    el = jnp.where((lane >= lo) & (lane < lo + EXPERTS_PER_GROUP), lg, ninf)
    m1 = rmax(el)
    i1 = rmin(jnp.where(el == m1, lane, big))
    el2 = jnp.where(lane == i1, ninf, el)
    m2 = rmax(el2)
    i2 = rmin(jnp.where(el2 == m2, lane, big))
    e2 = jnp.exp(m2 - m1)
    den = 1.0 + e2
    c1 = g_w * (1.0 / den)
    c2 = g_w * (e2 / den)
    route = jnp.where(lane == 0, i1 - N_GROUPS,
                      jnp.where(lane == 1, i2 - N_GROUPS,
                                jnp.where(lane == 2, c1, jnp.where(lane == 3, c2, 0.0))))
    route_ref[...] = route


def _outproj(mixa, mixb, wo, x2, g2, wr_hi, wr_lo, br):
    n_tok, d = x2.shape
    tm = TOKEN_TILE
    tok = lambda i: (i, 0)
    const = lambda i: (0, 0)
    return pl.pallas_call(
        _outproj_kernel,
        grid=(n_tok // tm,),
        in_specs=[
            pl.BlockSpec((tm, 512), tok),
            pl.BlockSpec((tm, 512), tok),
            pl.BlockSpec(wo.shape, const),
            pl.BlockSpec((tm, d), tok),
            pl.BlockSpec((1, d), const),
            pl.BlockSpec(wr_hi.shape, const),
            pl.BlockSpec(wr_lo.shape, const),
            pl.BlockSpec((1, LANES), const),
        ],
        out_specs=[pl.BlockSpec((tm, d), tok), pl.BlockSpec((tm, d), tok),
                   pl.BlockSpec((tm, LANES), tok)],
        out_shape=[jax.ShapeDtypeStruct((n_tok, d), F32),
                   jax.ShapeDtypeStruct((n_tok, d), BF16),
                   jax.ShapeDtypeStruct((n_tok, LANES), F32)],
        compiler_params=_cparams(1),
        name="outproj_router",
    )(mixa, mixb, wo, x2, g2, wr_hi, wr_lo, br)


def _experts_kernel(te_ref, nu_ref, x_ref, wg_ref, wu_ref, wd_ref, y_ref, wg_s, wu_s, wd_s):
    i = pl.program_id(0)

    @pl.when(i < nu_ref[0])
    def _():
        prev = te_ref[jnp.maximum(i - 1, 0)]
        new_expert = (i == 0) | (te_ref[i] != prev)

        @pl.when(new_expert)
        def _():
            wg_s[...] = wg_ref[0].astype(BF16)
            wu_s[...] = wu_ref[0].astype(BF16)
            wd_s[...] = wd_ref[0].astype(BF16)

        x = x_ref[...]
        g = jnp.dot(x, wg_s[...], preferred_element_type=F32)
        u = jnp.dot(x, wu_s[...], preferred_element_type=F32)
        hid = (g * jax.nn.sigmoid(g)) * u
        y = jnp.dot(hid.astype(BF16), wd_s[...], preferred_element_type=F32)
        y_ref[...] = y.astype(BF16)


def _experts(tile_expert, n_used, x_sorted, wg, wu, wd):
    p_rows, d = x_sorted.shape
    f = wg.shape[2]
    tm = EXPERT_TILE
    n_tiles = p_rows // tm
    live = lambda i, te, nu: jnp.minimum(i, nu[0] - 1)
    grid_spec = pltpu.PrefetchScalarGridSpec(
        num_scalar_prefetch=2,
        grid=(n_tiles,),
        in_specs=[
            pl.BlockSpec((tm, d), lambda i, te, nu: (live(i, te, nu), 0)),
            pl.BlockSpec((1, d, f), lambda i, te, nu: (te[live(i, te, nu)], 0, 0)),
            pl.BlockSpec((1, d, f), lambda i, te, nu: (te[live(i, te, nu)], 0, 0)),
            pl.BlockSpec((1, f, d), lambda i, te, nu: (te[live(i, te, nu)], 0, 0)),
        ],
        out_specs=pl.BlockSpec((tm, d), lambda i, te, nu: (live(i, te, nu), 0)),
        scratch_shapes=[pltpu.VMEM((d, f), BF16), pltpu.VMEM((d, f), BF16), pltpu.VMEM((f, d), BF16)],
    )
    return pl.pallas_call(
        _experts_kernel,
        grid_spec=grid_spec,
        out_shape=jax.ShapeDtypeStruct((p_rows, d), BF16),
        compiler_params=_cparams(1),
        name="experts",
    )(tile_expert, n_used, x_sorted, wg, wu, wd)


def _combine_kernel(h_ref, y_ref, route_ref, g_ref, o_ref):
    r = route_ref[...]
    c1 = r[:, 2:3]
    c2 = r[:, 3:4]
    h = h_ref[...] + c1 * y_ref[0].astype(F32) + c2 * y_ref[1].astype(F32)
    ms = jnp.mean(h * h, axis=-1, keepdims=True)
    o_ref[...] = h * lax.rsqrt(ms + EPS) * g_ref[...]


def _combine(h, y2, route, gf):
    n_tok, d = h.shape
    tm = TOKEN_TILE
    tok = lambda i: (i, 0)
    return pl.pallas_call(
        _combine_kernel,
        grid=(n_tok // tm,),
        in_specs=[
            pl.BlockSpec((tm, d), tok),
            pl.BlockSpec((2, tm, d), lambda i: (0, i, 0)),
            pl.BlockSpec((tm, LANES), tok),
            pl.BlockSpec((1, d), lambda i: (0, 0)),
        ],
        out_specs=pl.BlockSpec((tm, d), tok),
        out_shape=jax.ShapeDtypeStruct((n_tok, d), F32),
        compiler_params=_cparams(1),
        name="combine_final_norm",
    )(h, y2, route, gf)


def _rope_tables(seq):
    rows = seq // GRID_W
    r = jnp.repeat(jnp.arange(rows), GRID_W).astype(F32)
    c = jnp.tile(jnp.arange(GRID_W), rows).astype(F32)
    half = HEAD_DIM // 2
    inv = ROPE_THETA ** (-jnp.arange(0, half, 2, dtype=F32) / half)
    ar = r[:, None] * inv[None, :]
    ac = c[:, None] * inv[None, :]
    ang = jnp.concatenate([ar, ar, ac, ac], axis=-1)
    cos = jnp.cos(ang)
    sin = jnp.sin(ang)
    sign = np.where((np.arange(HEAD_DIM) % 32) < 16, -1.0, 1.0).astype(np.float32)
    sin_s = sin * sign[None, :]
    return jnp.tile(cos, (1, 2)), jnp.tile(sin_s, (1, 2))


def _t5_bucket(rel):
    nb = REL_BUCKETS // 2
    ret = (rel > 0).astype(jnp.int32) * nb
    n = jnp.abs(rel)
    max_exact = nb // 2
    large = max_exact + (jnp.log(jnp.maximum(n, 1).astype(F32) / max_exact)
                         / math.log(REL_MAX_DIST / max_exact) * (nb - max_exact)).astype(jnp.int32)
    large = jnp.minimum(large, nb - 1)
    return ret + jnp.where(n < max_exact, n, large)


def _window_bias(rel_bias):
    kq = 3 * Q_BLOCK
    rel = jnp.arange(kq)[None, :] - Q_BLOCK - jnp.arange(Q_BLOCK)[:, None]
    return rel_bias[_t5_bucket(rel)].astype(F32).transpose(2, 0, 1)


def _dispatch_plan(route, n_tok):
    tm = EXPERT_TILE
    n_assign = 2 * n_tok
    p_rows = n_assign + N_EXPERTS * tm
    eid = jnp.concatenate([route[:, 0], route[:, 1]]).astype(jnp.int32)
    onehot = (eid[:, None] == jnp.arange(N_EXPERTS)[None, :]).astype(jnp.int32)
    csum = jnp.cumsum(onehot, axis=0)
    rank = jnp.sum(onehot * csum, axis=1) - 1
    counts = csum[-1]
    padded = ((counts + tm - 1) // tm) * tm
    ends = jnp.cumsum(padded)
    starts = ends - padded
    pos = starts[eid] + rank
    src = jnp.zeros((p_rows,), jnp.int32).at[pos].set(jnp.arange(n_assign, dtype=jnp.int32) % n_tok)
    tile_start = jnp.arange(p_rows // tm, dtype=jnp.int32) * tm
    tile_expert = jnp.minimum(jnp.searchsorted(ends, tile_start, side="right"),
                              N_EXPERTS - 1).astype(jnp.int32)
    n_used = (ends[-1] // tm).astype(jnp.int32).reshape(1)
    return pos, src, tile_expert, n_used


def kernel(x, norm1_g, w_in, qn_g, kn_g, sink, rel_bias, gnorm_a, gnorm_b, w_o, norm2_g,
           w_group, b_group, w_expert, b_expert, w_gate, w_up, w_down, final_g):
    batch, seq, d = x.shape
    n_tok = batch * seq
    assert seq % TOKEN_TILE == 0 and seq % ATTN_Q_TILE == 0 and seq % GRID_W == 0
    assert norm1_g.shape[0] == 1, "single trunk layer"
    x2 = x.reshape(n_tok, d)

    w = w_in[0]
    qa_w, ka_w, va_w, qb_w, kb_w, vb_w = jnp.split(w, np.cumsum([512, 128, 128, 512, 128]).tolist(), axis=1)
    dup = lambda m: jnp.concatenate([m[:, :64], m[:, :64], m[:, 64:], m[:, 64:]], axis=1)
    w_cat = jnp.concatenate([qa_w, dup(ka_w), dup(va_w), qb_w, dup(kb_w), dup(vb_w)], axis=1).astype(BF16)

    cos, sin_s = _rope_tables(seq)
    qg = jnp.tile(qn_g[0], 2).reshape(1, LANES)
    kg = jnp.tile(kn_g[0], 2).reshape(1, LANES)
    lane = np.arange(LANES)
    bd = jnp.asarray(((lane[:, None] // HEAD_DIM) == (lane[None, :] // HEAD_DIM)) / HEAD_DIM, BF16)

    qa, kat, va, qb, kbt, vb = _inproj(x2, norm1_g, w_cat, cos, sin_s, qg, kg, bd, batch, seq)
    mixa = _attn_a(qa, kat, va, gnorm_a, batch, seq)
    mixb = _attn_b(qb, kbt, vb, sink[0], _window_bias(rel_bias), gnorm_b, batch, seq)

    wr = jnp.zeros((d, LANES), F32).at[:, :N_GROUPS].set(w_group[0]).at[:, N_GROUPS:N_GROUPS + N_EXPERTS].set(w_expert[0])
    br = jnp.zeros((1, LANES), F32).at[0, :N_GROUPS].set(b_group[0]).at[0, N_GROUPS:N_GROUPS + N_EXPERTS].set(b_expert[0])
    wr_hi = wr.astype(BF16)
    wr_lo = (wr - wr_hi.astype(F32)).astype(BF16)
    h, xn, route = _outproj(mixa, mixb, w_o[0].astype(BF16), x2, norm2_g, wr_hi, wr_lo, br)

    pos, src, tile_expert, n_used = _dispatch_plan(route, n_tok)
    f = w_gate.shape[-1]
    x_sorted = jnp.take(xn, src, axis=0)
    y_sorted = _experts(tile_expert, n_used, x_sorted,
                        w_gate.reshape(N_EXPERTS, d, f), w_up.reshape(N_EXPERTS, d, f),
                        w_down.reshape(N_EXPERTS, f, d))
    y2 = jnp.take(y_sorted, pos, axis=0).reshape(2, n_tok, d)
    out = _combine(h, y2, route, final_g.reshape(1, d))
    return out.reshape(batch, seq, d)
```
